```python
import math
import jax, jax.numpy as jnp
from jax import lax
import numpy as np

D_MODEL = 1024
BATCH = 1
SEQ = 16384
DEPTH = 2
DEC_BATCH = 32
DEC_SEQ = 8
PAST_LEN = 16384
PAGE_SIZE = 128

W_A = 256
A_BLOCKS = 4
A_BLOCK = W_A // A_BLOCKS
CONV_W = 4
RG_C = 8.0
H_B = 4
DK_B = 64
DV_B = 64
H_C = 4
HD_C = 64
H_IDX = 8
IDX_DIM = 64
TOPK_MAX = 256
Q_BLOCK = 128
H_D = 4
DK_D = 32
DV_D = 64
GLA_RANK = 16
GLA_TAU = 16.0
CHUNK = 64
N_BRANCH = 4
BRANCH_W = 256
D_FF = 2816
ROPE_THETA = 10000.0
DN_ALPHA = (2 * DEPTH) ** 0.25
DN_BETA = (8 * DEPTH) ** -0.25
NEG_BIG = -1e30
TINY = 1e-30
IN_SPLITS = (W_A, W_A,
             H_B * DK_B, H_B * DK_B, H_B * DV_B, H_B * DV_B,
             H_C * HD_C, H_C * HD_C, H_C * HD_C, H_IDX * IDX_DIM, IDX_DIM, H_IDX,
             H_D * DK_D, H_D * DK_D, H_D * DV_D, H_D * DV_D, GLA_RANK,
             N_BRANCH * D_MODEL)

kernel_name = 'hybrid_rglru_hgrn2_dsa_gla_step'

F32 = jnp.float32


def _split_points():
    return [int(s) for s in np.cumsum(np.array(IN_SPLITS))[:-1]]


def layer_norm(x, g, b, eps=1e-5):
    xf = x.astype(F32)
    mu = jnp.mean(xf, -1, keepdims=True)
    var = jnp.mean(jnp.square(xf - mu), -1, keepdims=True)
    return ((xf - mu) * lax.rsqrt(var + eps)).astype(x.dtype) * g + b


def head_rms(x, g, eps=1e-6):
    B, T = x.shape[:2]
    xf = x.astype(F32)
    y = xf * lax.rsqrt(jnp.mean(jnp.square(xf), -1, keepdims=True) + eps)
    return y.reshape(B, T, -1).astype(x.dtype) * g


def rope(x, pos):
    half = x.shape[-1] // 2
    inv = ROPE_THETA ** (-jnp.arange(half, dtype=F32) / half)
    ang = pos.astype(F32)[:, None] * inv[None, :]
    cos = jnp.cos(ang)[None, :, None, :]
    sin = jnp.sin(ang)[None, :, None, :]
    xf = x.astype(F32)
    x1, x2 = xf[..., :half], xf[..., half:]
    return jnp.concatenate([x1 * cos - x2 * sin, x2 * cos + x1 * sin], -1).astype(x.dtype)


def swiglu(h, wg, wu, wd):
    return (jax.nn.silu(h @ wg) * (h @ wu)) @ wd


def causal_conv(x, buf, w, b):
    T = x.shape[1]
    xp = jnp.concatenate([buf, x], axis=1)
    y = sum(xp[:, j:j + T] * w[j] for j in range(CONV_W)) + b
    return y, xp[:, -(CONV_W - 1):]


def rg_lru(x, wa, ba, wi, bi, lam, h0, reset_first):
    B, T, W = x.shape
    xb = x.reshape(B, T, A_BLOCKS, A_BLOCK)
    r = jax.nn.sigmoid(jnp.einsum('btgi,gij->btgj', xb, wa).reshape(B, T, W) + ba)
    i = jax.nn.sigmoid(jnp.einsum('btgi,gij->btgj', xb, wi).reshape(B, T, W) + bi)
    log_a = -RG_C * r.astype(F32) * jax.nn.softplus(-lam.astype(F32))
    a = jnp.exp(log_a)
    mult = jnp.sqrt(jnp.maximum(-jnp.expm1(2.0 * log_a), 0.0))
    if reset_first:
        mult = mult.at[:, 0].set(1.0)
    bterm = mult * (i * x).astype(F32)
    bterm = bterm.at[:, 0].add(a[:, 0] * h0.astype(F32))

    def combine(lhs, rhs):
        a1, b1 = lhs
        a2, b2 = rhs
        return a1 * a2, a2 * b1 + b2

    _, h = lax.associative_scan(combine, (a, bterm), axis=1)
    return h.astype(x.dtype), h[:, -1].astype(h0.dtype)


def chunk_gated_linear(q, k, v, log_f, s0):
    B, T, H, _ = q.shape
    dv = v.shape[-1]
    C = math.gcd(T, CHUNK)
    n = T // C

    def to_chunks(a):
        return a.astype(F32).reshape(B, n, C, H, a.shape[-1]).transpose(1, 0, 2, 3, 4)

    qs, ks, vs, gs = to_chunks(q), to_chunks(k), to_chunks(v), to_chunks(log_f)
    causal = jnp.tril(jnp.ones((C, C), bool))[None, :, :, None, None]

    def step(S, inp):
        qc, kc, vc, gc = inp
        b = jnp.cumsum(gc, axis=1)
        diff = b[:, :, None] - b[:, None, :]
        dec = jnp.where(causal, jnp.exp(jnp.where(causal, diff, 0.0)), 0.0)
        att = jnp.einsum('bthd,btshd,bshd->bhts', qc, dec, kc)
        o = jnp.einsum('bhts,bshv->bthv', att, vc) + jnp.einsum('bthd,bhdv->bthv', qc * jnp.exp(b), S)
        bl = b[:, -1]
        S = jnp.exp(bl)[..., None] * S + jnp.einsum('bshd,bshv->bhdv', kc * jnp.exp(bl[:, None] - b), vc)
        return S, o

    S, o = lax.scan(step, s0.astype(F32), (qs, ks, vs, gs))
    o = o.transpose(1, 0, 2, 3, 4).reshape(B, T, H, dv)
    return o.astype(q.dtype), S.astype(s0.dtype)


def indexer_scores(qi, ki, wi):
    s = jnp.einsum('bqhd,bkd->bqhk', qi, ki).astype(F32) * (IDX_DIM ** -0.5)
    return jnp.einsum('bqhk,bqh->bqk', jax.nn.relu(s), wi.astype(F32))


def select_keys(scores, admissible, topk):
    s = jnp.where(admissible, scores, NEG_BIG)
    vals, idx = lax.top_k(s, topk)
    return idx, vals > 0.5 * NEG_BIG


def sparse_attend(q, kg, vg, valid):
    s = jnp.einsum('bqhd,bqkhd->bqhk', q, kg).astype(F32) * (q.shape[-1] ** -0.5)
    s = jnp.where(valid[:, :, None, :], s, NEG_BIG)
    p = jax.nn.softmax(s, axis=-1)
    return jnp.einsum('bqhk,bqkhd->bqhd', p.astype(vg.dtype), vg)


def dsa_prompt(q, k, v, qi, ki, wi):
    B, T, H, hd = q.shape
    topk = min(TOPK_MAX, T // 4)
    nblk = T // Q_BLOCK
    bidx = jnp.arange(B)[:, None, None]
    key_pos = jnp.arange(T)

    def block(j):
        start = j * Q_BLOCK
        qb = lax.dynamic_slice_in_dim(q, start, Q_BLOCK, axis=1)
        qib = lax.dynamic_slice_in_dim(qi, start, Q_BLOCK, axis=1)
        wib = lax.dynamic_slice_in_dim(wi, start, Q_BLOCK, axis=1)
        sc = indexer_scores(qib, ki, wib)
        tq = start + jnp.arange(Q_BLOCK)
        adm = (key_pos[None, :] <= tq[:, None])[None]
        idx, valid = select_keys(sc, adm, topk)
        return sparse_attend(qb, k[bidx, idx], v[bidx, idx], valid)

    o = lax.map(block, jnp.arange(nblk))
    return o.transpose(1, 0, 2, 3, 4).reshape(B, T, H, hd)


def dsa_sample(q, k_new, v_new, qi, ki_new, wi, pool_k, pool_v, pool_ki, page_table):
    B, T, H, hd = q.shape
    past = page_table.shape[1] * PAGE_SIZE
    L = past + T
    topk = min(TOPK_MAX, L // 4)
    ki_past = pool_ki[page_table].reshape(B, past, IDX_DIM)
    ki_all = jnp.concatenate([ki_past, ki_new], axis=1)
    sc = indexer_scores(qi, ki_all, wi)
    tq = past + jnp.arange(T)
    adm = (jnp.arange(L)[None, :] <= tq[:, None])[None]
    idx, valid = select_keys(sc, adm, topk)
    bidx = jnp.arange(B)[:, None, None]
    in_past = idx < past
    pidx = jnp.minimum(idx, past - 1)
    phys = page_table[bidx, pidx // PAGE_SIZE]
    flat = phys * PAGE_SIZE + pidx % PAGE_SIZE
    pk = pool_k.reshape(-1, H, hd)[flat]
    pv = pool_v.reshape(-1, H, hd)[flat]
    nidx = jnp.clip(idx - past, 0, T - 1)
    kg = jnp.where(in_past[..., None, None], pk, k_new[bidx, nidx])
    vg = jnp.where(in_past[..., None, None], pv, v_new[bidx, nidx])
    return sparse_attend(q, kg, vg, valid)


def trunk_layer(x, c, pos, p, past):
    B, T, _ = x.shape
    prompt = past is None
    mod = (jax.nn.silu(c) @ p['w_mod'] + p['b_mod']).reshape(B, 9, D_MODEL)

    def modulate(h, j):
        return h * (1.0 + mod[:, 3 * j + 1][:, None]) + mod[:, 3 * j][:, None]

    def gate(j):
        return mod[:, 3 * j + 2][:, None]

    f = swiglu(modulate(x, 0), p['ffn_wg'][0], p['ffn_wu'][0], p['ffn_wd'][0])
    x = layer_norm(DN_ALPHA * x + 0.5 * gate(0) * f, p['ln_g'][0], p['ln_b'][0])

    u = modulate(x, 1)
    (xa, ga, qb, fb, ib, gb, qc, kc, vc, qi, ki, wi,
     qd, kd, vd, gd, lrd, mg) = jnp.split(u @ p['w_in'], _split_points(), axis=-1)

    conv_buf = jnp.zeros((B, CONV_W - 1, W_A), x.dtype) if prompt else past['conv']
    h0 = jnp.zeros((B, W_A), x.dtype) if prompt else past['h']
    xc, new_conv = causal_conv(xa, conv_buf, p['conv_w'], p['conv_b'])
    ra, new_h = rg_lru(xc, p['rg_wa'], p['rg_ba'], p['rg_wi'], p['rg_bi'], p['rg_lam'], h0, prompt)
    ya = ra * jax.nn.gelu(ga)

    sb0 = jnp.zeros((B, H_B, DK_B, DV_B), x.dtype) if prompt else past['sb']
    z = fb.reshape(B, T, H_B, DK_B).astype(F32)
    lb = p['hg_lb'].reshape(H_B, DK_B)
    f_gate = lb + (1.0 - lb) * jax.nn.sigmoid(z)
    log_f = jnp.log(jnp.maximum(f_gate, TINY))
    kb = ((1.0 - lb) * jax.nn.sigmoid(-z)).astype(x.dtype)
    ob, new_sb = chunk_gated_linear(jax.nn.silu(qb).reshape(B, T, H_B, DK_B), kb,
                                    ib.reshape(B, T, H_B, DV_B), log_f, sb0)
    yb = head_rms(ob, p['hg_norm']) * jax.nn.silu(gb)

    qc4 = rope(qc.reshape(B, T, H_C, HD_C), pos)
    kc4 = rope(kc.reshape(B, T, H_C, HD_C), pos)
    vc4 = vc.reshape(B, T, H_C, HD_C)
    qi4 = rope(qi.reshape(B, T, H_IDX, IDX_DIM), pos)
    ki1 = rope(ki[:, :, None, :], pos)[:, :, 0, :]
    wi_s = wi * (H_IDX ** -0.5)
    if prompt:
        oc = dsa_prompt(qc4, kc4, vc4, qi4, ki1, wi_s)
    else:
        oc = dsa_sample(qc4, kc4, vc4, qi4, ki1, wi_s, past['pool_k'], past['pool_v'],
                        past['pool_ki'], past['page_table'])
    yc = oc.reshape(B, T, H_C * HD_C)

    sd0 = jnp.zeros((B, H_D, DK_D, DV_D), x.dtype) if prompt else past['sd']
    log_alpha = jax.nn.log_sigmoid((lrd @ p['gla_wlr'] + p['gla_blr']).astype(F32)) / GLA_TAU
    od, new_sd = chunk_gated_linear(qd.reshape(B, T, H_D, DK_D) * (DK_D ** -0.5),
                                    kd.reshape(B, T, H_D, DK_D), vd.reshape(B, T, H_D, DV_D),
                                    log_alpha.reshape(B, T, H_D, DK_D), sd0)
    yd = head_rms(od, p['gla_norm']) * jax.nn.silu(gd)

    branches = jnp.stack([ya, yb, yc, yd], axis=2)
    proj = jnp.einsum('btjw,jwd->btjd', branches, p['w_br'])
    gates = jax.nn.sigmoid(mg.reshape(B, T, N_BRANCH, D_MODEL))
    mix = jnp.sum(gates * proj, axis=2) @ p['w_out']
    x = layer_norm(DN_ALPHA * x + gate(1) * mix, p['ln_g'][1], p['ln_b'][1])

    f = swiglu(modulate(x, 2), p['ffn_wg'][1], p['ffn_wu'][1], p['ffn_wd'][1])
    x = layer_norm(DN_ALPHA * x + 0.5 * gate(2) * f, p['ln_g'][2], p['ln_b'][2])
    return x, (kc4, vc4, ki1, new_conv, new_h, new_sb, new_sd)


def setup_inputs(seed: int = 0) -> dict:
    key = jax.random.key(seed)
    ks = iter(jax.random.split(key, 48))

    def nrm(shape, scale):
        return jax.random.normal(next(ks), shape, F32) * scale

    n_pages = PAST_LEN // PAGE_SIZE
    n_used = DEC_BATCH * n_pages
    n_pool = n_used + (n_used + 3) // 4
    page_table = jax.random.permutation(next(ks), n_pool)[:n_used].reshape(DEC_BATCH, n_pages).astype(jnp.int32)
    u = jax.random.uniform(next(ks), (DEPTH, W_A), F32, 0.9, 0.999)
    a_base = u ** (1.0 / RG_C)
    rg_lam = jnp.log(a_base) - jnp.log1p(-a_base)
    n_in = sum(IN_SPLITS)
    return {
        'x_prompt': nrm((BATCH, SEQ, D_MODEL), 1.0),
        'x_sample': nrm((DEC_BATCH, DEC_SEQ, D_MODEL), 1.0),
        'c_prompt': nrm((BATCH, D_MODEL), 1.0),
        'c_sample': nrm((DEC_BATCH, D_MODEL), 1.0),
        'cache_k': nrm((DEPTH, n_pool, PAGE_SIZE, H_C, HD_C), 1.0),
        'cache_v': nrm((DEPTH, n_pool, PAGE_SIZE, H_C, HD_C), 1.0),
        'cache_kidx': nrm((DEPTH, n_pool, PAGE_SIZE, IDX_DIM), 1.0),
        'page_table': page_table,
        'state_conv_a': nrm((DEPTH, DEC_BATCH, CONV_W - 1, W_A), 1.0),
        'state_h_a': nrm((DEPTH, DEC_BATCH, W_A), 0.5),
        'state_s_b': nrm((DEPTH, DEC_BATCH, H_B, DK_B, DV_B), 0.1),
        'state_s_d': nrm((DEPTH, DEC_BATCH, H_D, DK_D, DV_D), 0.1),
        'w_mod': nrm((DEPTH, D_MODEL, 9 * D_MODEL), 0.5 * D_MODEL ** -0.5),
        'b_mod': nrm((DEPTH, 9 * D_MODEL), 0.02),
        'ln_g': 1.0 + nrm((DEPTH, 3, D_MODEL), 0.02),
        'ln_b': nrm((DEPTH, 3, D_MODEL), 0.02),
        'ffn_wg': nrm((DEPTH, 2, D_MODEL, D_FF), D_MODEL ** -0.5),
        'ffn_wu': nrm((DEPTH, 2, D_MODEL, D_FF), D_MODEL ** -0.5),
        'ffn_wd': nrm((DEPTH, 2, D_FF, D_MODEL), DN_BETA * D_FF ** -0.5),
        'w_in': nrm((DEPTH, D_MODEL, n_in), D_MODEL ** -0.5),
        'conv_w': nrm((DEPTH, CONV_W, W_A), CONV_W ** -0.5),
        'conv_b': nrm((DEPTH, W_A), 0.02),
        'rg_wa': nrm((DEPTH, A_BLOCKS, A_BLOCK, A_BLOCK), A_BLOCK ** -0.5),
        'rg_ba': nrm((DEPTH, W_A), 0.02),
        'rg_wi': nrm((DEPTH, A_BLOCKS, A_BLOCK, A_BLOCK), A_BLOCK ** -0.5),
        'rg_bi': nrm((DEPTH, W_A), 0.02),
        'rg_lam': rg_lam,
        'hg_lb': nrm((DEPTH, H_B * DK_B), 1.0),
        'hg_norm': 1.0 + nrm((DEPTH, H_B * DV_B), 0.02),
        'gla_wlr': nrm((DEPTH, GLA_RANK, H_D * DK_D), GLA_RANK ** -0.5),
        'gla_blr': nrm((DEPTH, H_D * DK_D), 0.02),
        'gla_norm': 1.0 + nrm((DEPTH, H_D * DV_D), 0.02),
        'w_br': nrm((DEPTH, N_BRANCH, BRANCH_W, D_MODEL), BRANCH_W ** -0.5),
        'w_out': nrm((DEPTH, D_MODEL, D_MODEL), DN_BETA * D_MODEL ** -0.5),
    }


def reference(x_prompt, x_sample, c_prompt, c_sample, cache_k, cache_v, cache_kidx, page_table,
              state_conv_a, state_h_a, state_s_b, state_s_d, w_mod, b_mod, ln_g, ln_b,
              ffn_wg, ffn_wu, ffn_wd, w_in, conv_w, conv_b, rg_wa, rg_ba, rg_wi, rg_bi, rg_lam,
              hg_lb, hg_norm, gla_wlr, gla_blr, gla_norm, w_br, w_out):
    lbs = jax.nn.softmax(hg_lb.astype(F32), axis=0)
    lbs = jnp.cumsum(lbs, axis=0) - lbs[0]
    past_len = page_table.shape[1] * PAGE_SIZE
    pos_p = jnp.arange(x_prompt.shape[1], dtype=jnp.int32)
    pos_s = past_len + jnp.arange(x_sample.shape[1], dtype=jnp.int32)
    yp, ys = x_prompt, x_sample
    new_p, new_s = [], []
    for l in range(DEPTH):
        p = dict(w_mod=w_mod[l], b_mod=b_mod[l], ln_g=ln_g[l], ln_b=ln_b[l],
                 ffn_wg=ffn_wg[l], ffn_wu=ffn_wu[l], ffn_wd=ffn_wd[l], w_in=w_in[l],
                 conv_w=conv_w[l], conv_b=conv_b[l], rg_wa=rg_wa[l], rg_ba=rg_ba[l],
                 rg_wi=rg_wi[l], rg_bi=rg_bi[l], rg_lam=rg_lam[l], hg_lb=lbs[l],
                 hg_norm=hg_norm[l], gla_wlr=gla_wlr[l], gla_blr=gla_blr[l],
                 gla_norm=gla_norm[l], w_br=w_br[l], w_out=w_out[l])
        yp, st_p = trunk_layer(yp, c_prompt, pos_p, p, None)
        past = dict(pool_k=cache_k[l], pool_v=cache_v[l], pool_ki=cache_kidx[l],
                    page_table=page_table, conv=state_conv_a[l], h=state_h_a[l],
                    sb=state_s_b[l], sd=state_s_d[l])
        ys, st_s = trunk_layer(ys, c_sample, pos_s, p, past)
        new_p.append(st_p)
        new_s.append(st_s)

    def stk(sts, i):
        return jnp.stack([s[i] for s in sts])

    return (yp, ys,
            stk(new_p, 0), stk(new_p, 1), stk(new_p, 2), stk(new_p, 3), stk(new_p, 4), stk(new_p, 5), stk(new_p, 6),
            stk(new_s, 0), stk(new_s, 1), stk(new_s, 2), stk(new_s, 3), stk(new_s, 4), stk(new_s, 5), stk(new_s, 6))
```

```python
import functools

import numpy as np
import jax
import jax.numpy as jnp
from jax import lax
from jax.experimental import pallas as pl
from jax.experimental.pallas import tpu as pltpu

F32 = jnp.float32
BF16 = jnp.bfloat16
I32 = jnp.int32

D_MODEL = 1024
DEPTH = 2
PAGE_SIZE = 128
W_A = 256
A_BLOCKS = 4
CONV_W = 4
RG_C = 8.0
H_B, DK_B, DV_B = 4, 64, 64
H_C, HD_C = 4, 64
H_IDX, IDX_DIM = 8, 64
TOPK_MAX = 256
H_D, DK_D, DV_D = 4, 32, 64
GLA_RANK = 16
GLA_TAU = 16.0
CHUNK = 64
N_BRANCH = 4
BRANCH_W = 256
D_FF = 2816
ROPE_THETA = 10000.0
DN_ALPHA = (2 * DEPTH) ** 0.25
NEG_BIG = -1e30
TINY = 1e-30
LN_EPS = 1e-5
RMS_EPS = 1e-6
IN_SPLITS = (W_A, W_A,
             H_B * DK_B, H_B * DK_B, H_B * DV_B, H_B * DV_B,
             H_C * HD_C, H_C * HD_C, H_C * HD_C, H_IDX * IDX_DIM, IDX_DIM, H_IDX,
             H_D * DK_D, H_D * DK_D, H_D * DV_D, H_D * DV_D, GLA_RANK,
             N_BRANCH * D_MODEL)

LANES = 128
SUBLANES = 8
VMEM_LIMIT = 56 * 1024 * 1024

GRP_A = 2 * W_A
GRP_B = 4 * H_B * DK_B
GRP_C = 3 * H_C * HD_C
GRP_I = 640
GRP_D = 896
OFF_A = 0
OFF_B = OFF_A + GRP_A
OFF_C = OFF_B + GRP_B
OFF_I = OFF_C + GRP_C
OFF_D = OFF_I + GRP_I
N_MAIN = OFF_D + GRP_D
FF_CHUNK = 256
Q_TILE = 128
KEY_TILE = 512
PAGES_PER_STEP = 8


def _f32_key(v):
    b = int(np.array(v, np.float32).view(np.int32))
    return b ^ ((b >> 31) & 0x7FFFFFFF)


KEY_VALID_MIN = _f32_key(0.5 * NEG_BIG) + 1
INT_MIN = -2 ** 31


def _cparams(sem, vmem=VMEM_LIMIT):
    return pltpu.CompilerParams(dimension_semantics=sem, vmem_limit_bytes=vmem)


def _const_spec(shape):
    nd = len(shape)
    return pl.BlockSpec(shape, lambda *_: (0,) * nd, pipeline_mode=pl.Buffered(1))


def _sigmoid(x):
    return 1.0 / (1.0 + jnp.exp(-x))


def _silu(x):
    return x * _sigmoid(x)


def _softplus(x):
    return jnp.maximum(x, 0.0) + jnp.log1p(jnp.exp(-jnp.abs(x)))


def _layer_norm(y, g, b):
    mu = jnp.mean(y, axis=-1, keepdims=True)
    d = y - mu
    var = jnp.mean(d * d, axis=-1, keepdims=True)
    return d * lax.rsqrt(var + LN_EPS) * g + b


def _dot(a, b):
    return jnp.dot(a, b, preferred_element_type=F32)


def _dot_nt(a, b):
    return lax.dot_general(a, b, (((1,), (1,)), ((), ())), preferred_element_type=F32)


def _float_key(s):
    bits = pltpu.bitcast(s, I32)
    return bits ^ ((bits >> 31) & 0x7FFFFFFF)


def _mod_kernel(c_ref, w_ref, b_ref, o_ref):
    s = _silu(c_ref[...]).astype(BF16)
    o_ref[...] = _dot(s, w_ref[...]) + b_ref[...]


def _mod_call(c_pad, w_bf, b_row):
    rows = c_pad.shape[0]
    n = w_bf.shape[1]
    tn = 1152
    return pl.pallas_call(
        _mod_kernel,
        grid=(n // tn,),
        in_specs=[pl.BlockSpec((rows, D_MODEL), lambda j: (0, 0)),
                  pl.BlockSpec((D_MODEL, tn), lambda j: (0, j)),
                  pl.BlockSpec((1, tn), lambda j: (0, j))],
        out_specs=pl.BlockSpec((rows, tn), lambda j: (0, j)),
        out_shape=jax.ShapeDtypeStruct((rows, n), F32),
        compiler_params=_cparams(("arbitrary",)),
        name="mod",
    )(c_pad, w_bf, b_row)


def _mod_specs(per_row, tm, j):
    if per_row:
        return [pl.BlockSpec((tm, D_MODEL), functools.partial(lambda i, c: (i, c), c=3 * j + k))
                for k in range(3)]
    return [pl.BlockSpec((1, D_MODEL), functools.partial(lambda i, c: (0, c), c=3 * j + k))
            for k in range(3)]


def _ffn_kernel(x_ref, sh_ref, sc_ref, gt_ref, wg_ref, wu_ref, wd_ref, lg_ref, lb_ref,
                o_ref, acc_ref):
    x = x_ref[...]
    u = (x * (1.0 + sc_ref[...]) + sh_ref[...]).astype(BF16)
    acc_ref[...] = jnp.zeros_like(acc_ref)

    def body(c, carry):
        g = _dot(u, wg_ref[c])
        up = _dot(u, wu_ref[c])
        a = (_silu(g) * up).astype(BF16)
        acc_ref[...] += _dot(a, wd_ref[c])
        return carry

    lax.fori_loop(0, wg_ref.shape[0], body, 0)
    y = DN_ALPHA * x + (0.5 * gt_ref[...]) * acc_ref[...]
    o_ref[...] = _layer_norm(y, lg_ref[...], lb_ref[...])


def _ffn_call(x, mod, per_row, j, wg, wu, wd, lg, lb, tm):
    n = x.shape[0]
    nc = wg.shape[0]
    return pl.pallas_call(
        _ffn_kernel,
        grid=(n // tm,),
        in_specs=[pl.BlockSpec((tm, D_MODEL), lambda i: (i, 0))]
        + _mod_specs(per_row, tm, j)
        + [_const_spec((nc, D_MODEL, FF_CHUNK)), _const_spec((nc, D_MODEL, FF_CHUNK)),
           _const_spec((nc, FF_CHUNK, D_MODEL)), _const_spec((1, D_MODEL)), _const_spec((1, D_MODEL))],
        out_specs=pl.BlockSpec((tm, D_MODEL), lambda i: (i, 0)),
        out_shape=jax.ShapeDtypeStruct((n, D_MODEL), F32),
        scratch_shapes=[pltpu.VMEM((tm, D_MODEL), F32)],
        compiler_params=_cparams(("arbitrary",)),
        name="ffn",
    )(x, mod, mod, mod, wg, wu, wd, lg, lb)


def _rope(x, cos, sin):
    w = x.shape[1]
    lane = lax.broadcasted_iota(I32, x.shape, 1)
    first = (lane % HD_C) < (HD_C // 2)
    partner = jnp.where(first, pltpu.roll(x, w - HD_C // 2, 1), pltpu.roll(x, HD_C // 2, 1))
    return x * cos + partner * sin


def _inproj_kernel(x_ref, sh_ref, sc_ref, w_ref, cos_ref, sin_ref,
                   pa_ref, pb_ref, qc_ref, kcf_ref, vcf_ref, kcb_ref, vcb_ref,
                   qib_ref, kif_ref, kib_ref, wis_ref, pd_ref):
    x = x_ref[...]
    u = (x * (1.0 + sc_ref[...]) + sh_ref[...]).astype(BF16)
    cos = cos_ref[...]
    sin = sin_ref[...]
    pa_ref[...] = _dot(u, w_ref[:, OFF_A:OFF_B])
    pb_ref[...] = _dot(u, w_ref[:, OFF_B:OFF_C])
    pd_ref[...] = _dot(u, w_ref[:, OFF_D:N_MAIN])

    pc = _dot(u, w_ref[:, OFF_C:OFF_I])
    hc = H_C * HD_C
    cos4 = jnp.concatenate([cos] * (2 * hc // LANES), axis=1)
    sin4 = jnp.concatenate([sin] * (2 * hc // LANES), axis=1)
    qk = _rope(pc[:, :2 * hc], cos4, sin4)
    qc_ref[...] = (qk[:, :hc] * (HD_C ** -0.5)).astype(BF16)
    kcf_ref[...] = qk[:, hc:]
    kcb_ref[...] = qk[:, hc:].astype(BF16)
    vcf_ref[...] = pc[:, 2 * hc:]
    vcb_ref[...] = pc[:, 2 * hc:].astype(BF16)

    pi = _dot(u, w_ref[:, OFF_I:OFF_D])
    cos5 = jnp.concatenate([cos] * (GRP_I // LANES), axis=1)
    sin5 = jnp.concatenate([sin] * (GRP_I // LANES), axis=1)
    ri = _rope(pi, cos5, sin5)
    nqi = H_IDX * IDX_DIM
    qib_ref[...] = (ri[:, :nqi] * (IDX_DIM ** -0.5)).astype(BF16)
    ki = ri[:, nqi:nqi + IDX_DIM]
    kif_ref[...] = ki
    kib_ref[...] = ki.astype(BF16)
    wis_ref[...] = pi[:, nqi + IDX_DIM:nqi + IDX_DIM + H_IDX] * (H_IDX ** -0.5)


def _inproj_call(x, mod, per_row, w_main, cos, sin, tm):
    n = x.shape[0]
    hc = H_C * HD_C
    nqi = H_IDX * IDX_DIM

    def rows(w):
        return pl.BlockSpec((tm, w), lambda i: (i, 0))

    outs = [(GRP_A, F32), (GRP_B, F32), (hc, BF16), (hc, F32), (hc, F32), (hc, BF16), (hc, BF16),
            (nqi, BF16), (IDX_DIM, F32), (IDX_DIM, BF16), (H_IDX, F32), (GRP_D, F32)]
    return pl.pallas_call(
        _inproj_kernel,
        grid=(n // tm,),
        in_specs=[rows(D_MODEL)] + _mod_specs(per_row, tm, 1)[:2]
        + [_const_spec((D_MODEL, N_MAIN)), rows(LANES), rows(LANES)],
        out_specs=[rows(w) for w, _ in outs],
        out_shape=[jax.ShapeDtypeStruct((n, w), dt) for w, dt in outs],
        compiler_params=_cparams(("arbitrary",)),
        name="inproj",
    )(x, mod, mod, w_main, cos, sin)


def _mixa_kernel(reset_first, pa_ref, conv0_ref, h0_ref, cw_ref, cb_ref, wa_ref, ba_ref,
                 wi_ref, bi_ref, lam_ref, ya_ref, convo_ref, ho_ref, ext_ref, h_ref):
    t = pl.program_id(1)
    r_rows = pa_ref.shape[0]

    @pl.when(t == 0)
    def _():
        ext_ref[0:SUBLANES, :] = conv0_ref[0]
        h_ref[...] = h0_ref[0]

    xa = pa_ref[:, 0:W_A]
    ga = pa_ref[:, W_A:2 * W_A]
    ext_ref[SUBLANES:, :] = xa
    base = SUBLANES - (CONV_W - 1)
    xc = ext_ref[base:base + r_rows, :] * cw_ref[0:1, :]
    for j in range(1, CONV_W):
        xc = xc + ext_ref[base + j:base + j + r_rows, :] * cw_ref[j:j + 1, :]
    xc = xc + cb_ref[...]
    carry_rows = ext_ref[r_rows:r_rows + SUBLANES, :]
    ext_ref[0:SUBLANES, :] = carry_rows
    convo_ref[0] = carry_rows

    xcb = xc.astype(BF16)
    r = _sigmoid(_dot(xcb, wa_ref[...]) + ba_ref[...])
    i = _sigmoid(_dot(xcb, wi_ref[...]) + bi_ref[...])
    log_a = (-RG_C) * r * _softplus(-lam_ref[...])
    a = jnp.exp(log_a)
    mult = jnp.sqrt(jnp.maximum(-jnp.tanh(log_a) * (jnp.exp(2.0 * log_a) + 1.0), 0.0))
    row = lax.broadcasted_iota(I32, (r_rows, W_A), 0)
    if reset_first:
        mult = jnp.where((row == 0) & (t == 0), 1.0, mult)
    b = mult * (i * xc)

    sh = 1
    while sh < r_rows:
        a_s = pltpu.roll(a, sh, 0)
        b_s = pltpu.roll(b, sh, 0)
        ok = row >= sh
        b = jnp.where(ok, a * b_s + b, b)
        a = jnp.where(ok, a * a_s, a)
        sh *= 2
    h = a * h_ref[...] + b
    h_last = h[r_rows - 1:r_rows, :]
    h_ref[...] = h_last
    ho_ref[0] = h_last
    gelu = 0.5 * ga * (1.0 + jnp.tanh(0.7978845608028654 * (ga + 0.044715 * (ga * ga * ga))))
    ya_ref[...] = h * gelu


def _mixa_call(pa, conv0, h0, cw, cb, wa, ba, wi, bi, lam, nb, nt, r_rows, reset_first):
    n = pa.shape[0]
    return pl.pallas_call(
        functools.partial(_mixa_kernel, reset_first),
        grid=(nb, nt),
        in_specs=[pl.BlockSpec((r_rows, GRP_A), lambda b, t: (b * nt + t, 0)),
                  pl.BlockSpec((1, SUBLANES, W_A), lambda b, t: (b, 0, 0)),
                  pl.BlockSpec((1, 1, W_A), lambda b, t: (b, 0, 0)),
                  _const_spec((CONV_W, W_A)), _const_spec((1, W_A)),
                  _const_spec((W_A, W_A)), _const_spec((1, W_A)),
                  _const_spec((W_A, W_A)), _const_spec((1, W_A)), _const_spec((1, W_A))],
        out_specs=[pl.BlockSpec((r_rows, W_A), lambda b, t: (b * nt + t, 0)),
                   pl.BlockSpec((1, SUBLANES, W_A), lambda b, t: (b, 0, 0)),
                   pl.BlockSpec((1, 1, W_A), lambda b, t: (b, 0, 0))],
        out_shape=[jax.ShapeDtypeStruct((n, W_A), F32),
                   jax.ShapeDtypeStruct((nb, SUBLANES, W_A), F32),
                   jax.ShapeDtypeStruct((nb, 1, W_A), F32)],
        scratch_shapes=[pltpu.VMEM((r_rows + SUBLANES, W_A), F32), pltpu.VMEM((1, W_A), F32)],
        compiler_params=_cparams(("arbitrary", "arbitrary")),
        name="mix_a",
    )(pa, conv0, h0, cw, cb, wa, ba, wi, bi, lam)


def _gl_core(q, k, v, g, gate, norm, dk, dv, chunk, st_ref, y_ref):
    r_rows, lk = q.shape
    lv = v.shape[1]
    row = lax.broadcasted_iota(I32, (r_rows, 1), 0)
    rc = row % chunk

    b = g
    sh = 1
    while sh < chunk:
        b = b + jnp.where(rc >= sh, pltpu.roll(b, sh, 0), 0.0)
        sh *= 2

    ones_blk = (lax.broadcasted_iota(I32, (lk, lv), 0) // dk
                == lax.broadcasted_iota(I32, (lk, lv), 1) // dv)
    ones_bf = jnp.where(ones_blk, 1.0, 0.0).astype(BF16)

    def band(d, carry):
        k_d, b_d, v_d, acc = carry
        arg = jnp.where(rc >= d, b - b_d, NEG_BIG)
        term = (q * k_d * jnp.exp(arg)).astype(BF16)
        acc = acc + _dot(term, ones_bf) * v_d
        return (pltpu.roll(k_d, 1, 0), pltpu.roll(b_d, 1, 0), pltpu.roll(v_d, 1, 0), acc)

    _, _, _, o = lax.fori_loop(0, chunk, band, (k, b, v, jnp.zeros((r_rows, lv), F32)))

    blk_mask = jnp.where(lax.broadcasted_iota(I32, (lv, lk), 0) // dv
                         == lax.broadcasted_iota(I32, (lv, lk), 1) // dk, 1.0, 0.0)
    outs = []
    for c in range(r_rows // chunk):
        lo = c * chunk
        bc = b[lo:lo + chunk]
        b_last = bc[chunk - 1:chunk, :]
        st = st_ref[...]
        qt = (q[lo:lo + chunk] * jnp.exp(bc)).astype(BF16)
        outs.append(o[lo:lo + chunk] + _dot_nt(qt, st.astype(BF16)))
        kt = (k[lo:lo + chunk] * jnp.exp(b_last - bc)).astype(BF16)
        upd = lax.dot_general(v[lo:lo + chunk].astype(BF16), kt, (((0,), (0,)), ((), ())),
                              preferred_element_type=F32)
        st_ref[...] = st * jnp.exp(b_last) + upd * blk_mask
    o = outs[0] if len(outs) == 1 else jnp.concatenate(outs, axis=0)

    o2 = o * o
    ones_vv = jnp.where(lax.broadcasted_iota(I32, (lv, lv), 0) // dv
                        == lax.broadcasted_iota(I32, (lv, lv), 1) // dv, 1.0, 0.0).astype(BF16)
    hi = o2.astype(BF16)
    lo2 = (o2 - hi.astype(F32)).astype(BF16)
    ms = (_dot(hi, ones_vv) + _dot(lo2, ones_vv)) * (1.0 / dv)
    y_ref[...] = (o * lax.rsqrt(ms + RMS_EPS)) * norm * _silu(gate)


def _mixb_kernel(layer, chunk, pb_ref, s0_ref, lb_ref, norm_ref, y_ref, so_ref, st_ref):
    t = pl.program_id(1)

    @pl.when(t == 0)
    def _():
        st_ref[...] = s0_ref[0]

    w = H_B * DK_B
    raw = lb_ref[...]
    e = jnp.exp(raw - jnp.max(raw, axis=0, keepdims=True))
    sm = e / jnp.sum(e, axis=0, keepdims=True)
    lb = jnp.sum(sm[0:layer + 1], axis=0, keepdims=True) - sm[0:1]
    z = pb_ref[:, w:2 * w]
    sig = _sigmoid(z)
    q = _silu(pb_ref[:, 0:w])
    k = (1.0 - lb) * _sigmoid(-z)
    g = jnp.log(jnp.maximum(lb + (1.0 - lb) * sig, TINY))
    _gl_core(q, k, pb_ref[:, 2 * w:3 * w], g, pb_ref[:, 3 * w:4 * w], norm_ref[...],
             DK_B, DV_B, chunk, st_ref, y_ref)
    so_ref[0] = st_ref[...]


def _mixd_kernel(chunk, pd_ref, s0_ref, wlr_ref, blr_ref, norm_ref, y_ref, so_ref, st_ref):
    t = pl.program_id(1)

    @pl.when(t == 0)
    def _():
        st_ref[...] = s0_ref[0]

    wk = H_D * DK_D
    wv = H_D * DV_D
    q = pd_ref[:, 0:wk] * (DK_D ** -0.5)
    k = pd_ref[:, wk:2 * wk]
    v = pd_ref[:, 2 * wk:2 * wk + wv]
    gate = pd_ref[:, 2 * wk + wv:2 * wk + 2 * wv]
    lr = pd_ref[:, 2 * wk + 2 * wv:GRP_D].astype(BF16)
    la = _dot(lr, wlr_ref[...]) + blr_ref[...]
    g = (-_softplus(-la)) * (1.0 / GLA_TAU)
    _gl_core(q, k, v, g, gate, norm_ref[...], DK_D, DV_D, chunk, st_ref, y_ref)
    so_ref[0] = st_ref[...]


def _mix_gl_call(kern, p, s0, params, width, lk, lv, nb, nt, r_rows, name):
    n = p.shape[0]
    return pl.pallas_call(
        kern,
        grid=(nb, nt),
        in_specs=[pl.BlockSpec((r_rows, width), lambda b, t: (b * nt + t, 0)),
                  pl.BlockSpec((1, lv, lk), lambda b, t: (b, 0, 0))]
        + [_const_spec(a.shape) for a in params],
        out_specs=[pl.BlockSpec((r_rows, lv), lambda b, t: (b * nt + t, 0)),
                   pl.BlockSpec((1, lv, lk), lambda b, t: (b, 0, 0))],
        out_shape=[jax.ShapeDtypeStruct((n, lv), F32),
                   jax.ShapeDtypeStruct((nb, lv, lk), F32)],
        scratch_shapes=[pltpu.VMEM((lv, lk), F32)],
        compiler_params=_cparams(("arbitrary", "arbitrary")),
        name=name,
    )(p, s0, *params)


def _kth_largest_key(count_ge, rows, topk):
    def bit_body(i, cu):
        trial = cu | lax.shift_left(jnp.int32(1), 31 - i)
        cnt = count_ge(trial ^ INT_MIN)
        return jnp.where(cnt >= topk, trial, cu)

    cu = lax.fori_loop(0, 32, bit_body, jnp.zeros((rows, 1), I32))
    return cu ^ INT_MIN


def _head_masked_rows(q, heads, hd):
    lane = lax.broadcasted_iota(I32, q.shape, 1)
    return jnp.concatenate([jnp.where(lane // hd == h, q, jnp.zeros_like(q)) for h in range(heads)],
                           axis=0)


def _softmax_step(s, sel, m, l, acc, vblk):
    s = jnp.where(sel, s, NEG_BIG)
    m_new = jnp.maximum(m, jnp.max(s, axis=1, keepdims=True))
    p = jnp.where(sel, jnp.exp(s - m_new), 0.0)
    alpha = jnp.exp(m - m_new)
    l = alpha * l + jnp.sum(p, axis=1, keepdims=True)
    acc = alpha * acc + _dot(p.astype(BF16), vblk)
    return m_new, l, acc


def _merge_heads(accs, ls, heads, hd):
    lane = lax.broadcasted_iota(I32, accs[0].shape, 1)
    out = jnp.zeros_like(accs[0])
    for h in range(heads):
        out = out + jnp.where(lane // hd == h, accs[h] / ls[h], 0.0)
    return out


def _dsa_prompt_kernel(topk, qc_ref, qi_ref, wi_ref, ki_ref, kc_ref, vc_ref, o_ref, keys_ref):
    j = pl.program_id(0)
    nkb = ((j + 1) * Q_TILE + KEY_TILE - 1) // KEY_TILE
    qpos = j * Q_TILE + lax.broadcasted_iota(I32, (Q_TILE, 1), 0)

    qstack = jnp.concatenate([qi_ref[:, h * IDX_DIM:(h + 1) * IDX_DIM] for h in range(H_IDX)], axis=0)
    wis = wi_ref[...]
    wcols = [jnp.broadcast_to(wis[:, h:h + 1], (Q_TILE, KEY_TILE)) for h in range(H_IDX)]

    def score_body(kb, carry):
        start = pl.multiple_of(kb * KEY_TILE, KEY_TILE)
        s_all = _dot_nt(qstack, ki_ref[pl.ds(start, KEY_TILE), :])
        acc = jnp.zeros((Q_TILE, KEY_TILE), F32)
        for h in range(H_IDX):
            acc = acc + jnp.maximum(s_all[h * Q_TILE:(h + 1) * Q_TILE], 0.0) * wcols[h]
        kpos = start + lax.broadcasted_iota(I32, (1, KEY_TILE), 1)
        keys_ref[kb] = _float_key(jnp.where(kpos <= qpos, acc, NEG_BIG))
        return carry

    lax.fori_loop(0, nkb, score_body, 0)

    def count_ge(c):
        def body(kb, part):
            m = jnp.where(keys_ref[kb] >= c, 1, 0)
            for cc in range(KEY_TILE // LANES):
                part = part + m[:, cc * LANES:(cc + 1) * LANES]
            return part

        part = lax.fori_loop(0, nkb, body, jnp.zeros((Q_TILE, LANES), I32))
        return jnp.sum(part, axis=1, keepdims=True)

    thr = jnp.maximum(_kth_largest_key(count_ge, Q_TILE, topk), KEY_VALID_MIN)

    qs = _head_masked_rows(qc_ref[...], H_C, HD_C)

    def att_body(kb, carry):
        start = pl.multiple_of(kb * KEY_TILE, KEY_TILE)
        s_all = _dot_nt(qs, kc_ref[pl.ds(start, KEY_TILE), :])
        vblk = vc_ref[pl.ds(start, KEY_TILE), :]
        sel = keys_ref[kb] >= thr
        new = []
        for h in range(H_C):
            m, l, acc = carry[h]
            new.append(_softmax_step(s_all[h * Q_TILE:(h + 1) * Q_TILE], sel, m, l, acc, vblk))
        return tuple(new)

    init = tuple((jnp.full((Q_TILE, 1), NEG_BIG, F32), jnp.zeros((Q_TILE, 1), F32),
                  jnp.zeros((Q_TILE, H_C * HD_C), F32)) for _ in range(H_C))
    res = lax.fori_loop(0, nkb, att_body, init)
    o_ref[...] = _merge_heads([r[2] for r in res], [r[1] for r in res], H_C, HD_C)


def _dsa_prompt_call(qc, qib, wis, kib, kcb, vcb, topk):
    t = qc.shape[0]
    hc = H_C * HD_C
    return pl.pallas_call(
        functools.partial(_dsa_prompt_kernel, topk),
        grid=(t // Q_TILE,),
        in_specs=[pl.BlockSpec((Q_TILE, hc), lambda j: (j, 0)),
                  pl.BlockSpec((Q_TILE, H_IDX * IDX_DIM), lambda j: (j, 0)),
                  pl.BlockSpec((Q_TILE, H_IDX), lambda j: (j, 0)),
                  _const_spec((t, IDX_DIM)), _const_spec((t, hc)), _const_spec((t, hc))],
        out_specs=pl.BlockSpec((Q_TILE, hc), lambda j: (j, 0)),
        out_shape=jax.ShapeDtypeStruct((t, hc), F32),
        scratch_shapes=[pltpu.VMEM((t // KEY_TILE, Q_TILE, KEY_TILE), I32)],
        compiler_params=_cparams(("arbitrary",)),
        name="dsa_prompt",
    )(qc, qib, wis, kib, kcb, vcb)


def _dsa_sample_kernel(topk, n_pages, pt_ref, qc_ref, qi_ref, wi_ref, kin_ref, kcn_ref, vcn_ref,
                       pki_ref, pk_ref, pv_ref, o_ref,
                       kibuf, kbuf, vbuf, keys_ref, keyn_ref, sem_i, sem_k, sem_v):
    b = pl.program_id(0)
    t_new = qc_ref.shape[1]
    n_steps = n_pages // PAGES_PER_STEP
    step_keys = PAGES_PER_STEP * PAGE_SIZE
    hc = H_C * HD_C

    def ki_copy(p):
        return pltpu.make_async_copy(pki_ref.at[pt_ref[b, p]], kibuf.at[p], sem_i)

    def kv_copies(step, slot, p):
        page = pt_ref[b, step * PAGES_PER_STEP + p]
        return (pltpu.make_async_copy(pk_ref.at[page], kbuf.at[slot, p], sem_k.at[slot]),
                pltpu.make_async_copy(pv_ref.at[page], vbuf.at[slot, p], sem_v.at[slot]))

    def kv_start(step, slot):
        for p in range(PAGES_PER_STEP):
            ck, cv = kv_copies(step, slot, p)
            ck.start()
            cv.start()

    def kv_wait(step, slot):
        for p in range(PAGES_PER_STEP):
            ck, cv = kv_copies(step, slot, p)
            ck.wait()
            cv.wait()

    def start_ki(p, c):
        ki_copy(p).start()
        return c

    lax.fori_loop(0, n_pages, start_ki, 0)
    kv_start(0, 0)

    def wait_ki(p, c):
        ki_copy(p).wait()
        return c

    lax.fori_loop(0, n_pages, wait_ki, 0)

    qi = qi_ref[0].astype(BF16)
    wis = wi_ref[0]
    qstack = jnp.concatenate([qi[:, h * IDX_DIM:(h + 1) * IDX_DIM] for h in range(H_IDX)], axis=0)
    wstack = jnp.concatenate([wis[:, h:h + 1] for h in range(H_IDX)], axis=0)

    def index_scores(kblk):
        s_all = jnp.maximum(_dot_nt(qstack, kblk), 0.0) * wstack
        acc = jnp.zeros((t_new, kblk.shape[0]), F32)
        for h in range(H_IDX):
            acc = acc + s_all[h * t_new:(h + 1) * t_new]
        return acc

    def score_body(st, carry):
        blk = kibuf[pl.ds(st * PAGES_PER_STEP, PAGES_PER_STEP)]
        kblk = blk.reshape(step_keys, IDX_DIM).astype(BF16)
        keys_ref[st] = _float_key(index_scores(kblk))
        return carry

    lax.fori_loop(0, n_steps, score_body, 0)

    pad = jnp.zeros((LANES - t_new, IDX_DIM), F32)
    kin = jnp.concatenate([kin_ref[0], pad], axis=0).astype(BF16)
    sn = index_scores(kin)
    col = lax.broadcasted_iota(I32, (t_new, LANES), 1)
    rowi = lax.broadcasted_iota(I32, (t_new, LANES), 0)
    keyn = jnp.where(col < t_new, _float_key(jnp.where(col <= rowi, sn, NEG_BIG)), INT_MIN)
    keyn_ref[...] = keyn

    def count_ge(c):
        def body(st, part):
            m = jnp.where(keys_ref[st] >= c, 1, 0)
            for cc in range(step_keys // LANES):
                part = part + m[:, cc * LANES:(cc + 1) * LANES]
            return part

        part = lax.fori_loop(0, n_steps, body, jnp.where(keyn_ref[...] >= c, 1, 0))
        return jnp.sum(part, axis=1, keepdims=True)

    thr = jnp.maximum(_kth_largest_key(count_ge, t_new, topk), KEY_VALID_MIN)

    qs = _head_masked_rows(qc_ref[0].astype(BF16), H_C, HD_C)

    def att_heads(carry, s_all, sel, vblk):
        new = []
        for h in range(H_C):
            m, l, acc = carry[h]
            new.append(_softmax_step(s_all[h * t_new:(h + 1) * t_new], sel, m, l, acc, vblk))
        return tuple(new)

    def att_body(st, carry):
        slot = st % 2

        @pl.when(st + 1 < n_steps)
        def _():
            kv_start(st + 1, 1 - slot)

        kv_wait(st, slot)
        kblk = kbuf[slot].reshape(step_keys, hc).astype(BF16)
        vblk = vbuf[slot].reshape(step_keys, hc).astype(BF16)
        return att_heads(carry, _dot_nt(qs, kblk), keys_ref[st] >= thr, vblk)

    init = tuple((jnp.full((t_new, 1), NEG_BIG, F32), jnp.zeros((t_new, 1), F32),
                  jnp.zeros((t_new, hc), F32)) for _ in range(H_C))
    carry = lax.fori_loop(0, n_steps, att_body, init)

    padv = jnp.zeros((LANES - t_new, hc), F32)
    kcn = jnp.concatenate([kcn_ref[0], padv], axis=0).astype(BF16)
    vcn = jnp.concatenate([vcn_ref[0], padv], axis=0).astype(BF16)
    carry = att_heads(carry, _dot_nt(qs, kcn), keyn_ref[...] >= thr, vcn)
    o_ref[0] = _merge_heads([r[2] for r in carry], [r[1] for r in carry], H_C, HD_C)


def _dsa_sample_call(page_table, qc3, qi3, wi3, kin3, kcn3, vcn3, pool_ki, pool_k, pool_v, topk):
    nb, t_new, hc = qc3.shape
    n_pages = page_table.shape[1]
    n_steps = n_pages // PAGES_PER_STEP

    def seq(w):
        return pl.BlockSpec((1, t_new, w), lambda b, pt: (b, 0, 0))

    grid_spec = pltpu.PrefetchScalarGridSpec(
        num_scalar_prefetch=1,
        grid=(nb,),
        in_specs=[seq(hc), seq(H_IDX * IDX_DIM), seq(H_IDX), seq(IDX_DIM), seq(hc), seq(hc),
                  pl.BlockSpec(memory_space=pl.ANY), pl.BlockSpec(memory_space=pl.ANY),
                  pl.BlockSpec(memory_space=pl.ANY)],
        out_specs=seq(hc),
        scratch_shapes=[pltpu.VMEM((n_pages, PAGE_SIZE, IDX_DIM), F32),
                        pltpu.VMEM((2, PAGES_PER_STEP, PAGE_SIZE, hc), F32),
                        pltpu.VMEM((2, PAGES_PER_STEP, PAGE_SIZE, hc), F32),
                        pltpu.VMEM((n_steps, t_new, PAGES_PER_STEP * PAGE_SIZE), I32),
                        pltpu.VMEM((t_new, LANES), I32),
                        pltpu.SemaphoreType.DMA(()),
                        pltpu.SemaphoreType.DMA((2,)),
                        pltpu.SemaphoreType.DMA((2,))],
    )
    return pl.pallas_call(
        functools.partial(_dsa_sample_kernel, topk, n_pages),
        grid_spec=grid_spec,
        out_shape=jax.ShapeDtypeStruct((nb, t_new, hc), F32),
        compiler_params=_cparams(("arbitrary",)),
        name="dsa_sample",
    )(page_table, qc3, qi3, wi3, kin3, kcn3, vcn3, pool_ki, pool_k, pool_v)


def _merge_kernel(x_ref, sh_ref, sc_ref, gt_ref, ya_ref, yb_ref, yc_ref, yd_ref,
                  wmg_ref, wbr_ref, wo_ref, lg_ref, lb_ref, o_ref):
    x = x_ref[...]
    u = (x * (1.0 + sc_ref[...]) + sh_ref[...]).astype(BF16)
    mix = jnp.zeros(x.shape, F32)
    for j, y_ref in enumerate((ya_ref, yb_ref, yc_ref, yd_ref)):
        gate = _sigmoid(_dot(u, wmg_ref[j]))
        mix = mix + gate * _dot(y_ref[...].astype(BF16), wbr_ref[j])
    out = _dot(mix.astype(BF16), wo_ref[...])
    o_ref[...] = _layer_norm(DN_ALPHA * x + gt_ref[...] * out, lg_ref[...], lb_ref[...])


def _merge_call(x, mod, per_row, ys, wmg, wbr, wo, lg, lb, tm):
    n = x.shape[0]

    def rows(w):
        return pl.BlockSpec((tm, w), lambda i: (i, 0))

    return pl.pallas_call(
        _merge_kernel,
        grid=(n // tm,),
        in_specs=[rows(D_MODEL)] + _mod_specs(per_row, tm, 1) + [rows(BRANCH_W)] * N_BRANCH
        + [_const_spec((N_BRANCH, D_MODEL, D_MODEL)), _const_spec((N_BRANCH, BRANCH_W, D_MODEL)),
           _const_spec((D_MODEL, D_MODEL)), _const_spec((1, D_MODEL)), _const_spec((1, D_MODEL))],
        out_specs=rows(D_MODEL),
        out_shape=jax.ShapeDtypeStruct((n, D_MODEL), F32),
        compiler_params=_cparams(("arbitrary",)),
        name="merge",
    )(x, mod, mod, mod, *ys, wmg, wbr, wo, lg, lb)


def _block_diag(w):
    g, a, b = w.shape
    out = jnp.zeros((g * a, g * b), w.dtype)
    for i in range(g):
        out = out.at[i * a:(i + 1) * a, i * b:(i + 1) * b].set(w[i])
    return out


def _state_to_blocks(s):
    bsz, h, dk, dv = s.shape
    out = jnp.zeros((bsz, h * dv, h * dk), s.dtype)
    for i in range(h):
        out = out.at[:, i * dv:(i + 1) * dv, i * dk:(i + 1) * dk].set(jnp.swapaxes(s[:, i], 1, 2))
    return out


def _blocks_to_state(st, h, dk, dv):
    return jnp.stack([jnp.swapaxes(st[:, i * dv:(i + 1) * dv, i * dk:(i + 1) * dk], 1, 2)
                      for i in range(h)], axis=1)


def _rope_tables(pos):
    half = HD_C // 2
    inv = ROPE_THETA ** (-jnp.arange(half, dtype=F32) / half)
    ang = pos.astype(F32)[:, None] * inv[None, :]
    cos = jnp.cos(ang)
    sin = jnp.sin(ang)
    cos_t = jnp.concatenate([cos, cos] * (LANES // HD_C), axis=1)
    sin_t = jnp.concatenate([-sin, sin] * (LANES // HD_C), axis=1)
    return cos_t, sin_t


def _layer_weights(l, w_mod, b_mod, ln_g, ln_b, ffn_wg, ffn_wu, ffn_wd, w_in, conv_w, conv_b,
                   rg_wa, rg_ba, rg_wi, rg_bi, rg_lam, hg_lb, hg_norm, gla_wlr, gla_blr,
                   gla_norm, w_br, w_out):
    nc = D_FF // FF_CHUNK
    offs = np.concatenate([[0], np.cumsum(IN_SPLITS)])
    n_i = int(offs[12] - offs[9])
    n_d = int(offs[17] - offs[12])
    win = w_in[l]
    w_main = jnp.concatenate(
        [win[:, :offs[12]], jnp.zeros((D_MODEL, GRP_I - n_i), F32),
         win[:, offs[12]:offs[17]], jnp.zeros((D_MODEL, GRP_D - n_d), F32)], axis=1).astype(BF16)
    lr_rows = GRP_D - (2 * H_D * DK_D + 2 * H_D * DV_D)
    wlr = jnp.concatenate([gla_wlr[l], jnp.zeros((lr_rows - GLA_RANK, H_D * DK_D), F32)],
                          axis=0).astype(BF16)
    return dict(
        ffn_wg=[ffn_wg[l, i].reshape(D_MODEL, nc, FF_CHUNK).transpose(1, 0, 2).astype(BF16) for i in range(2)],
        ffn_wu=[ffn_wu[l, i].reshape(D_MODEL, nc, FF_CHUNK).transpose(1, 0, 2).astype(BF16) for i in range(2)],
        ffn_wd=[ffn_wd[l, i].reshape(nc, FF_CHUNK, D_MODEL).astype(BF16) for i in range(2)],
        ln_g=[ln_g[l, i][None] for i in range(3)],
        ln_b=[ln_b[l, i][None] for i in range(3)],
        w_main=w_main,
        w_mg=win[:, offs[17]:].reshape(D_MODEL, N_BRANCH, D_MODEL).transpose(1, 0, 2).astype(BF16),
        conv_w=conv_w[l], conv_b=conv_b[l][None],
        rg_wa=_block_diag(rg_wa[l]).astype(BF16), rg_ba=rg_ba[l][None],
        rg_wi=_block_diag(rg_wi[l]).astype(BF16), rg_bi=rg_bi[l][None],
        rg_lam=rg_lam[l][None],
        hg_lb=hg_lb, hg_norm=hg_norm[l][None],
        gla_wlr=wlr, gla_blr=gla_blr[l][None], gla_norm=gla_norm[l][None],
        w_br=w_br[l].astype(BF16), w_out=w_out[l].astype(BF16),
    )


def _trunk_layer(l, x, nb, nt_tok, mod, per_row, cos, sin, w, past, tiles):
    tm_ffn, tm_proj, tm_merge, r_rows = tiles
    n = x.shape[0]
    x = _ffn_call(x, mod, per_row, 0, w['ffn_wg'][0], w['ffn_wu'][0], w['ffn_wd'][0],
                  w['ln_g'][0], w['ln_b'][0], tm_ffn)
    (pa, pb, qc, kcf, vcf, kcb, vcb, qib, kif, kib, wis, pd) = _inproj_call(
        x, mod, per_row, w['w_main'], cos, sin, tm_proj)

    nt = nt_tok // r_rows
    chunk = min(CHUNK, r_rows)
    if past is None:
        conv0 = jnp.zeros((nb, SUBLANES, W_A), F32)
        h0 = jnp.zeros((nb, 1, W_A), F32)
        sb0 = jnp.zeros((nb, H_B * DV_B, H_B * DK_B), F32)
        sd0 = jnp.zeros((nb, H_D * DV_D, H_D * DK_D), F32)
    else:
        conv0 = jnp.pad(past['conv'], ((0, 0), (SUBLANES - (CONV_W - 1), 0), (0, 0)))
        h0 = past['h'][:, None, :]
        sb0 = _state_to_blocks(past['sb'])
        sd0 = _state_to_blocks(past['sd'])

    ya, conv_o, h_o = _mixa_call(pa, conv0, h0, w['conv_w'], w['conv_b'], w['rg_wa'], w['rg_ba'],
                                 w['rg_wi'], w['rg_bi'], w['rg_lam'], nb, nt, r_rows, past is None)
    yb, sb_o = _mix_gl_call(functools.partial(_mixb_kernel, l, chunk), pb, sb0,
                            (w['hg_lb'], w['hg_norm']), GRP_B, H_B * DK_B, H_B * DV_B,
                            nb, nt, r_rows, "mix_b")
    yd, sd_o = _mix_gl_call(functools.partial(_mixd_kernel, chunk), pd, sd0,
                            (w['gla_wlr'], w['gla_blr'], w['gla_norm']), GRP_D, H_D * DK_D,
                            H_D * DV_D, nb, nt, r_rows, "mix_d")

    hc = H_C * HD_C
    if past is None:
        yc = _dsa_prompt_call(qc, qib, wis, kib, kcb, vcb, min(TOPK_MAX, nt_tok // 4))
    else:
        n_keys = past['page_table'].shape[1] * PAGE_SIZE + nt_tok
        yc = _dsa_sample_call(
            past['page_table'], qc.reshape(nb, nt_tok, hc), qib.reshape(nb, nt_tok, -1),
            wis.reshape(nb, nt_tok, -1), kif.reshape(nb, nt_tok, -1), kcf.reshape(nb, nt_tok, hc),
            vcf.reshape(nb, nt_tok, hc), past['pool_ki'], past['pool_k'], past['pool_v'],
            min(TOPK_MAX, n_keys // 4)).reshape(n, hc)

    x = _merge_call(x, mod, per_row, (ya, yb, yc, yd), w['w_mg'], w['w_br'], w['w_out'],
                    w['ln_g'][1], w['ln_b'][1], tm_merge)
    x = _ffn_call(x, mod, per_row, 2, w['ffn_wg'][1], w['ffn_wu'][1], w['ffn_wd'][1],
                  w['ln_g'][2], w['ln_b'][2], tm_ffn)
    state = (kcf.reshape(nb, nt_tok, H_C, HD_C), vcf.reshape(nb, nt_tok, H_C, HD_C),
             kif.reshape(nb, nt_tok, IDX_DIM), conv_o[:, SUBLANES - (CONV_W - 1):, :], h_o[:, 0, :],
             _blocks_to_state(sb_o, H_B, DK_B, DV_B), _blocks_to_state(sd_o, H_D, DK_D, DV_D))
    return x, state


def kernel(x_prompt, x_sample, c_prompt, c_sample, cache_k, cache_v, cache_kidx, page_table, state_conv_a, state_h_a, state_s_b, state_s_d, w_mod, b_mod, ln_g, ln_b, ffn_wg, ffn_wu, ffn_wd, w_in, conv_w, conv_b, rg_wa, rg_ba, rg_wi, rg_bi, rg_lam, hg_lb, hg_norm, gla_wlr, gla_blr, gla_norm, w_br, w_out):
    bp, tp, _ = x_prompt.shape
    bs, ts, _ = x_sample.shape
    past_len = page_table.shape[1] * PAGE_SIZE
    hc = H_C * HD_C

    cos_p, sin_p = _rope_tables(jnp.arange(tp, dtype=I32))
    cos_s, sin_s = _rope_tables(past_len + jnp.arange(ts, dtype=I32))
    cos_p, sin_p = jnp.tile(cos_p, (bp, 1)), jnp.tile(sin_p, (bp, 1))
    cos_s, sin_s = jnp.tile(cos_s, (bs, 1)), jnp.tile(sin_s, (bs, 1))

    c_all = jnp.concatenate([c_prompt, c_sample], axis=0)
    rows = c_all.shape[0]
    rows_pad = -(-rows // SUBLANES) * SUBLANES
    c_pad = jnp.pad(c_all, ((0, rows_pad - rows), (0, 0)))

    tiles_p = (min(512, tp), min(512, tp), min(256, tp), min(256, tp))
    tiles_s = (bs * ts, bs * ts, bs * ts, ts)

    yp = x_prompt.reshape(bp * tp, D_MODEL)
    ys = x_sample.reshape(bs * ts, D_MODEL)
    new_p, new_s = [], []
    for l in range(DEPTH):
        w = _layer_weights(l, w_mod, b_mod, ln_g, ln_b, ffn_wg, ffn_wu, ffn_wd, w_in, conv_w,
                           conv_b, rg_wa, rg_ba, rg_wi, rg_bi, rg_lam, hg_lb, hg_norm, gla_wlr,
                           gla_blr, gla_norm, w_br, w_out)
        mod_all = _mod_call(c_pad, w_mod[l].astype(BF16), b_mod[l][None])
        mod_p = mod_all[:bp]
        mod_s = jnp.repeat(mod_all[bp:bp + bs], ts, axis=0)
        yp, st_p = _trunk_layer(l, yp, bp, tp, mod_p, bp > 1, cos_p, sin_p, w, None, tiles_p)
        past = dict(pool_k=cache_k[l].reshape(-1, PAGE_SIZE, hc),
                    pool_v=cache_v[l].reshape(-1, PAGE_SIZE, hc),
                    pool_ki=cache_kidx[l], page_table=page_table,
                    conv=state_conv_a[l], h=state_h_a[l], sb=state_s_b[l], sd=state_s_d[l])
        ys, st_s = _trunk_layer(l, ys, bs, ts, mod_s, True, cos_s, sin_s, w, past, tiles_s)
        new_p.append(st_p)
        new_s.append(st_s)

    def stk(sts, i):
        return jnp.stack([s[i] for s in sts])

    return ((yp.reshape(bp, tp, D_MODEL), ys.reshape(bs, ts, D_MODEL))
            + tuple(stk(new_p, i) for i in range(7)) + tuple(stk(new_s, i) for i in range(7)))
```
